```python
import math
import jax
import jax.numpy as jnp
from jax import lax
import numpy as np

D_MODEL = 2048
BATCH = 4
SEQ = 2048
DEPTH = 4

GRID_W = 64
CTX_LEN = 256
EPS = 1e-6
ROPE_BASE = 10000.0
NEG_INF = -1e30
DA_HEADS = 8
DA_DH = 64
DA_QBLOCK = 128
NA_HEADS = 8
NA_DH = 128
NA_ROWS_MAX = 8
NA_COLS = 16
NA_QCOLS = 16
NA_BAND = NA_QCOLS + NA_COLS
RK_HEADS = 16
RK_DH = 64
RK_DECAY_RANK = 64
RK_A_RANK = 64
RK_GATE_RANK = 128
RK_CONV = 3
RK_LN_EPS = 64e-5
W_A = DA_HEADS * 2 * DA_DH
W_B = NA_HEADS * NA_DH
W_C = RK_HEADS * RK_DH
W_BRANCH = W_A
D_IN = 3 * W_A + 3 * W_B + 3 * W_C
N_BRANCH = 3
N_MOD = 6
PEER_HEADS = 8
PEER_NKEYS = 128
PEER_N = PEER_NKEYS * PEER_NKEYS
PEER_QDIM = 256
PEER_TOPK = 16
PEER_BLOCK = 128

kernel_name = 'hybrid_diffusion_diffattn_natten_rwkv7_peer'


def rmsnorm(x, g):
    xf = x.astype(jnp.float32)
    return (xf * lax.rsqrt(jnp.mean(xf * xf, axis=-1, keepdims=True) + EPS)).astype(x.dtype) * g


def modulate(h, shift, scale):
    return h * (1.0 + scale) + shift


def axial_rope(n_tok):
    t = jnp.arange(n_tok)
    n_freq = DA_DH // 4
    inv = ROPE_BASE ** (-jnp.arange(n_freq, dtype=jnp.float32) / n_freq)
    ang_r = (t // GRID_W).astype(jnp.float32)[:, None] * inv
    ang_c = (t % GRID_W).astype(jnp.float32)[:, None] * inv
    ang = jnp.concatenate([ang_r, ang_r, ang_c, ang_c], axis=-1)
    return jnp.cos(ang), jnp.sin(ang)


def rotate_half(y):
    h = y.shape[-1] // 2
    return jnp.concatenate([-y[..., h:], y[..., :h]], axis=-1)


def rope2d(x, cos, sin):
    half = DA_DH // 2
    rot = jnp.concatenate([rotate_half(x[..., :half]), rotate_half(x[..., half:])], axis=-1)
    cos = cos[:, None, None, :].astype(x.dtype)
    sin = sin[:, None, None, :].astype(x.dtype)
    return x * cos + rot * sin


def softmax_attn(q, k, v):
    s = jnp.einsum('bqhd,bkhd->bhqk', q, k).astype(jnp.float32) * (q.shape[-1] ** -0.5)
    p = jax.nn.softmax(s, axis=-1).astype(v.dtype)
    return jnp.einsum('bhqk,bkhd->bqhd', p, v)


def diff_core(q, k, v, lam):
    s = jnp.einsum('bqhmd,bkhmd->bhmqk', q, k).astype(jnp.float32) * (DA_DH ** -0.5)
    p = jax.nn.softmax(s, axis=-1)
    a = (p[:, :, 0] - lam * p[:, :, 1]).astype(v.dtype)
    return jnp.einsum('bhqk,bkhe->bqhe', a, v)


def diff_attention(q_l, k_l, v_l, q_c, k_c, v_c, cos, sin, q_g, k_g, lq1, lk1, lq2, lk2, sub_g, lam_init, need_ctx):
    B, S = q_l.shape[:2]
    split = lambda z: z.reshape(*z.shape[:-1], DA_HEADS, 2, DA_DH)
    vh = lambda z: z.reshape(*z.shape[:-1], DA_HEADS, 2 * DA_DH)
    lam = (jnp.exp(jnp.sum(lq1 * lk1).astype(jnp.float32))
           - jnp.exp(jnp.sum(lq2 * lk2).astype(jnp.float32)) + lam_init)
    kc = rmsnorm(split(k_c), k_g)
    vc = vh(v_c)
    ql = rope2d(rmsnorm(split(q_l), q_g), cos, sin)
    kl = rope2d(rmsnorm(split(k_l), k_g), cos, sin)
    k_all = jnp.concatenate([kl, kc], axis=1)
    v_all = jnp.concatenate([vh(v_l), vc], axis=1)
    nb = S // DA_QBLOCK
    qb = jnp.moveaxis(ql.reshape(B, nb, DA_QBLOCK, DA_HEADS, 2, DA_DH), 1, 0)
    o_l = lax.map(lambda qblk: diff_core(qblk, k_all, v_all, lam), qb)
    o_l = jnp.moveaxis(o_l, 0, 1).reshape(B, S, DA_HEADS, 2 * DA_DH)

    def post(o):
        return (rmsnorm(o, sub_g) * (1.0 - lam_init)).reshape(*o.shape[:2], W_A)

    o_c = post(diff_core(rmsnorm(split(q_c), q_g), kc, vc, lam)) if need_ctx else None
    return post(o_l), o_c


def neighbourhood_attention(q_l, k_l, v_l, q_c, k_c, v_c, q_g, k_g, rpb, need_ctx):
    B, S = q_l.shape[:2]
    rows = S // GRID_W
    kr = min(NA_ROWS_MAX, rows)
    ncb = GRID_W // NA_QCOLS
    scale = NA_DH ** -0.5
    heads = lambda z: z.reshape(*z.shape[:-1], NA_HEADS, NA_DH)
    grid = lambda z: z.reshape(B, rows, GRID_W, NA_HEADS, NA_DH)
    kc = rmsnorm(heads(k_c), k_g)
    vc = heads(v_c)
    qg = grid(rmsnorm(heads(q_l), q_g))
    kg = grid(rmsnorm(heads(k_l), k_g))
    vg = grid(heads(v_l))
    qcol = np.arange(GRID_W).reshape(ncb, NA_QCOLS)
    band = (np.clip(np.arange(ncb) * NA_QCOLS - NA_COLS // 2, 0, GRID_W - NA_BAND)[:, None]
            + np.arange(NA_BAND))
    win0 = np.clip(qcol - NA_COLS // 2, 0, GRID_W - NA_COLS)[..., None]
    kcol = band[:, None, :]
    col_ok = (kcol >= win0) & (kcol < win0 + NA_COLS)
    dc_idx = np.clip(kcol - qcol[..., None] + NA_COLS - 1, 0, 2 * NA_COLS - 2)
    rpb_cols = rpb[:, :, dc_idx]
    nwin = kr * NA_BAND

    def one_row(args):
        r, q_r = args
        rs = jnp.clip(r - kr // 2, 0, rows - kr)
        k_b = lax.dynamic_slice_in_dim(kg, rs, kr, axis=1)[:, :, band]
        v_b = lax.dynamic_slice_in_dim(vg, rs, kr, axis=1)[:, :, band]
        q_b = q_r.reshape(B, ncb, NA_QCOLS, NA_HEADS, NA_DH)
        bias = jnp.take(rpb_cols, rs + jnp.arange(kr) - r + NA_ROWS_MAX - 1, axis=1)
        bias = bias.transpose(0, 2, 3, 1, 4)
        s_win = jnp.einsum('bjqhd,brjkhd->bhjqrk', q_b, k_b).astype(jnp.float32) * scale + bias
        s_win = jnp.where(col_ok[:, :, None, :], s_win, NEG_INF)
        s_ctx = jnp.einsum('bjqhd,bkhd->bhjqk', q_b, kc).astype(jnp.float32) * scale
        s = jnp.concatenate([s_win.reshape(*s_win.shape[:4], nwin), s_ctx], axis=-1)
        p = jax.nn.softmax(s, axis=-1).astype(v_b.dtype)
        o = (jnp.einsum('bhjqrk,brjkhd->bjqhd', p[..., :nwin].reshape(s_win.shape), v_b)
             + jnp.einsum('bhjqk,bkhd->bjqhd', p[..., nwin:], vc))
        return o.reshape(B, GRID_W, NA_HEADS, NA_DH)

    o_l = lax.map(one_row, (jnp.arange(rows), jnp.moveaxis(qg, 1, 0)))
    o_l = jnp.moveaxis(o_l, 0, 1).reshape(B, S, W_B)
    o_c = None
    if need_ctx:
        o_c = softmax_attn(rmsnorm(heads(q_c), q_g), kc, vc).reshape(B, -1, W_B)
    return o_l, o_c


def rk_heads(z):
    return z.reshape(*z.shape[:-1], RK_HEADS, RK_DH)


def centred_conv(x, w):
    pad = RK_CONV // 2
    T = x.shape[1]
    xp = jnp.pad(x, ((0, 0), (pad, pad), (0, 0)))
    y = xp[:, 0:T] * w[0]
    for j in range(1, RK_CONV):
        y = y + xp[:, j:j + T] * w[j]
    return y


def rwkv_side(h, rkv, conv_w, w0, w1, w2, a0, a1, a2, k_k, k_a):
    r, k, v = jnp.split(centred_conv(rkv, conv_w), 3, axis=-1)
    lw = jnp.einsum('nbtr,nrc->nbtc', jnp.tanh(jnp.einsum('btd,ndr->nbtr', h, w1)), w2) + w0[:, None, None]
    w = jnp.exp(-jnp.exp(-jax.nn.softplus(-lw.astype(jnp.float32)) - 0.5))
    a = jax.nn.sigmoid((jnp.einsum('nbtr,nrc->nbtc', jnp.einsum('btd,ndr->nbtr', h, a1), a2)
                        + a0[:, None, None]).astype(jnp.float32))
    kk = rk_heads((k * k_k).astype(jnp.float32))
    kk = kk * lax.rsqrt(jnp.sum(kk * kk, axis=-1, keepdims=True) + 1e-12)
    kd = k[None] * (1.0 + (a - 1.0) * k_a)
    return rk_heads(r), rk_heads(v), rk_heads(w), rk_heads(a), rk_heads(kd), kk


def orient(z):
    return jnp.stack([z[0], jnp.flip(z[1], axis=1)])


def orient_shared(z):
    return jnp.stack([z, jnp.flip(z, axis=1)])


def rwkv_scan(S0, r, w, k, v, kk, a, emit):
    xs = tuple(jnp.moveaxis(z.astype(jnp.float32), 2, 0) for z in (r, w, k, v, kk, a))

    def step(S, inp):
        r_t, w_t, k_t, v_t, kk_t, a_t = inp
        sa = jnp.einsum('dbhvk,dbhk->dbhv', S, -kk_t)
        S = S * w_t[..., None, :] + sa[..., None] * (kk_t * a_t)[..., None, :] + v_t[..., None] * k_t[..., None, :]
        return S, (jnp.einsum('dbhvk,dbhk->dbhv', S, r_t) if emit else None)

    return lax.scan(step, S0, xs)


def rwkv_out(ys, h, r, kd, v, g1, g2, r_k, ln_w, ln_b):
    y = ys[:, 0] + jnp.flip(ys[:, 1], axis=0)
    y = jnp.moveaxis(y, 0, 1)
    mu = jnp.mean(y, axis=-1, keepdims=True)
    var = jnp.mean(jnp.square(y - mu), axis=-1, keepdims=True)
    y = (y - mu) * lax.rsqrt(var + RK_LN_EPS) * rk_heads(ln_w) + rk_heads(ln_b)
    bonus = jnp.einsum('bthn,dbthn,hn->bth', r, kd, r_k)[..., None] * v
    g = jax.nn.sigmoid(h @ g1) @ g2
    return ((y + bonus).reshape(*h.shape[:2], W_C) * g).astype(h.dtype)


def rwkv_mixer(h_l, h_c, rkv_l, rkv_c, conv_w, w0, w1, w2, a0, a1, a2, g1, g2, k_k, k_a, r_k, ln_w, ln_b, need_ctx):
    side_c = rwkv_side(h_c, rkv_c, conv_w, w0, w1, w2, a0, a1, a2, k_k, k_a)
    side_l = rwkv_side(h_l, rkv_l, conv_w, w0, w1, w2, a0, a1, a2, k_k, k_a)

    def run(S, side, emit):
        r, v, w, a, kd, kk = side
        return rwkv_scan(S, orient_shared(r), orient(w), orient(kd), orient_shared(v), orient_shared(kk),
                         orient(a), emit)

    S0 = jnp.zeros((2, h_l.shape[0], RK_HEADS, RK_DH, RK_DH), jnp.float32)
    S_c, ys_c = run(S0, side_c, need_ctx)
    _, ys_l = run(S_c, side_l, True)
    o_l = rwkv_out(ys_l, h_l, side_l[0], side_l[4], side_l[1], g1, g2, r_k, ln_w, ln_b)
    o_c = rwkv_out(ys_c, h_c, side_c[0], side_c[4], side_c[1], g1, g2, r_k, ln_w, ln_b) if need_ctx else None
    return o_l, o_c


def merge(h, ya, yb, yc, w_gate, b_gate, w_br, w_o):
    gates = jax.nn.sigmoid(h @ w_gate + b_gate).reshape(*h.shape[:-1], N_BRANCH, h.shape[-1])
    br = jnp.einsum('btnc,ncd->btnd', jnp.stack([ya, yb, yc], axis=-2), w_br)
    return jnp.sum(gates * br, axis=-2) @ w_o


def peer(h, q_w, sub_keys, u_tab, v_tab):
    n_tok, d = h.shape
    q = (h @ q_w).reshape(n_tok, PEER_HEADS, 2, PEER_QDIM // 2)
    s = jnp.einsum('thpd,hpnd->thpn', q, sub_keys).astype(jnp.float32)
    s_top, i_top = lax.top_k(s, PEER_TOPK)
    cand_s = (s_top[:, :, 0, :, None] + s_top[:, :, 1, None, :]).reshape(n_tok, PEER_HEADS, PEER_TOPK * PEER_TOPK)
    cand_i = (i_top[:, :, 0, :, None] * PEER_NKEYS + i_top[:, :, 1, None, :]).reshape(n_tok, PEER_HEADS, PEER_TOPK * PEER_TOPK)
    best_s, pos = lax.top_k(cand_s, PEER_TOPK)
    idx = jnp.take_along_axis(cand_i, pos, axis=-1)
    gate = jax.nn.softmax(best_s, axis=-1)

    def block(args):
        xb, ib, gb = args
        act = jax.nn.gelu(jnp.einsum('td,thkd->thk', xb, u_tab[ib]))
        return jnp.einsum('thk,thkd->td', (gb * act).astype(xb.dtype), v_tab[ib])

    nb = n_tok // PEER_BLOCK
    out = lax.map(block, (h.reshape(nb, PEER_BLOCK, d),
                          idx.reshape(nb, PEER_BLOCK, PEER_HEADS, PEER_TOPK),
                          gate.reshape(nb, PEER_BLOCK, PEER_HEADS, PEER_TOPK)))
    return out.reshape(n_tok, d)


def setup_inputs(seed: int = 0) -> dict:
    key = jax.random.key(seed)
    keys = iter(jax.random.split(key, 64))
    L, D = DEPTH, D_MODEL

    def nrm(shape, scale):
        return jax.random.normal(next(keys), shape, jnp.float32) * scale

    def gain(shape):
        return 1.0 + nrm(shape, 0.02)

    conv_centre = jnp.zeros((RK_CONV, 1), jnp.float32).at[RK_CONV // 2].set(1.0)
    return {
        'x': nrm((BATCH, SEQ, D), 1.0),
        'c': nrm((BATCH, D), 1.0),
        'ctx': nrm((BATCH, CTX_LEN, D), 1.0),
        'c_ctx': nrm((D,), 1.0),
        'ada_w': nrm((L, D, N_MOD * D), 0.5 * D ** -0.5),
        'ada_b': nrm((L, N_MOD * D), 0.02),
        'norm1_g': gain((L, D)),
        'norm2_g': gain((L, D)),
        'w_in': nrm((L, D, D_IN), D ** -0.5),
        'da_qn': gain((L, DA_DH)),
        'da_kn': gain((L, DA_DH)),
        'da_lq1': nrm((L, DA_DH), 0.1),
        'da_lk1': nrm((L, DA_DH), 0.1),
        'da_lq2': nrm((L, DA_DH), 0.1),
        'da_lk2': nrm((L, DA_DH), 0.1),
        'da_subln': gain((L, 2 * DA_DH)),
        'na_qn': gain((L, NA_DH)),
        'na_kn': gain((L, NA_DH)),
        'na_rpb': nrm((L, NA_HEADS, 2 * NA_ROWS_MAX - 1, 2 * NA_COLS - 1), 0.1),
        'rk_conv': nrm((L, RK_CONV, 3 * W_C), 0.1) + conv_centre,
        'rk_w0': nrm((L, 2, W_C), 0.5) - 0.5,
        'rk_w1': nrm((L, 2, D, RK_DECAY_RANK), D ** -0.5),
        'rk_w2': nrm((L, 2, RK_DECAY_RANK, W_C), 0.1),
        'rk_a0': nrm((L, 2, W_C), 0.5),
        'rk_a1': nrm((L, 2, D, RK_A_RANK), D ** -0.5),
        'rk_a2': nrm((L, 2, RK_A_RANK, W_C), 0.1),
        'rk_g1': nrm((L, D, RK_GATE_RANK), D ** -0.5),
        'rk_g2': nrm((L, RK_GATE_RANK, W_C), RK_GATE_RANK ** -0.5),
        'rk_kk': 1.0 + nrm((L, W_C), 0.1),
        'rk_ka': 1.0 + nrm((L, W_C), 0.1),
        'rk_rk': nrm((L, RK_HEADS, RK_DH), 0.1),
        'rk_lnw': gain((L, W_C)),
        'rk_lnb': nrm((L, W_C), 0.02),
        'w_gate': nrm((L, D, N_BRANCH * D), D ** -0.5),
        'b_gate': nrm((L, N_BRANCH * D), 0.02),
        'w_br': nrm((L, N_BRANCH, W_BRANCH, D), W_BRANCH ** -0.5),
        'w_o': nrm((L, D, D), D ** -0.5),
        'peer_q': nrm((L, D, PEER_HEADS * PEER_QDIM), D ** -0.5),
        'peer_keys': nrm((L, PEER_HEADS, 2, PEER_NKEYS, PEER_QDIM // 2), (PEER_QDIM // 2) ** -0.5),
        'peer_u': nrm((L, PEER_N, D), D ** -0.5),
        'peer_v': nrm((L, PEER_N, D), 0.5),
    }


def reference(x, c, ctx, c_ctx, ada_w, ada_b, norm1_g, norm2_g, w_in, da_qn, da_kn, da_lq1, da_lk1, da_lq2,
              da_lk2, da_subln, na_qn, na_kn, na_rpb, rk_conv, rk_w0, rk_w1, rk_w2, rk_a0, rk_a1, rk_a2, rk_g1,
              rk_g2, rk_kk, rk_ka, rk_rk, rk_lnw, rk_lnb, w_gate, b_gate, w_br, w_o, peer_q, peer_keys, peer_u,
              peer_v):
    B, S, D = x.shape
    C = ctx.shape[1]
    cos, sin = axial_rope(S)
    silu_c = jax.nn.silu(c)
    silu_cc = jax.nn.silu(c_ctx)
    offs = [W_A, 2 * W_A, 3 * W_A, 3 * W_A + W_B, 3 * W_A + 2 * W_B, 3 * W_A + 3 * W_B]
    x_lat, x_ctx = x, ctx
    for l in range(DEPTH):
        need_ctx = l < DEPTH - 1
        lam_init = 0.8 - 0.6 * math.exp(-0.3 * l)
        mod = (silu_c @ ada_w[l] + ada_b[l]).reshape(B, N_MOD, 1, D)
        mod_c = (silu_cc @ ada_w[l] + ada_b[l]).reshape(N_MOD, 1, D)
        h_l = modulate(rmsnorm(x_lat, norm1_g[l]), mod[:, 0], mod[:, 1])
        h_c = modulate(rmsnorm(x_ctx, norm1_g[l]), mod_c[0], mod_c[1])
        aq_l, ak_l, av_l, bq_l, bk_l, bv_l, rkv_l = jnp.split(h_l @ w_in[l], offs, axis=-1)
        aq_c, ak_c, av_c, bq_c, bk_c, bv_c, rkv_c = jnp.split(h_c @ w_in[l], offs, axis=-1)
        ya_l, ya_c = diff_attention(aq_l, ak_l, av_l, aq_c, ak_c, av_c, cos, sin, da_qn[l], da_kn[l], da_lq1[l],
                                    da_lk1[l], da_lq2[l], da_lk2[l], da_subln[l], lam_init, need_ctx)
        yb_l, yb_c = neighbourhood_attention(bq_l, bk_l, bv_l, bq_c, bk_c, bv_c, na_qn[l], na_kn[l], na_rpb[l],
                                             need_ctx)
        yc_l, yc_c = rwkv_mixer(h_l, h_c, rkv_l, rkv_c, rk_conv[l], rk_w0[l], rk_w1[l], rk_w2[l], rk_a0[l],
                                rk_a1[l], rk_a2[l], rk_g1[l], rk_g2[l], rk_kk[l], rk_ka[l], rk_rk[l], rk_lnw[l],
                                rk_lnb[l], need_ctx)
        x_lat = x_lat + mod[:, 2] * merge(h_l, ya_l, yb_l, yc_l, w_gate[l], b_gate[l], w_br[l], w_o[l])
        if need_ctx:
            x_ctx = x_ctx + mod_c[2] * merge(h_c, ya_c, yb_c, yc_c, w_gate[l], b_gate[l], w_br[l], w_o[l])
        h2_l = modulate(rmsnorm(x_lat, norm2_g[l]), mod[:, 3], mod[:, 4])
        if need_ctx:
            h2_c = modulate(rmsnorm(x_ctx, norm2_g[l]), mod_c[3], mod_c[4])
            tok = jnp.concatenate([h2_c, h2_l], axis=1).reshape(B * (C + S), D)
            f = peer(tok, peer_q[l], peer_keys[l], peer_u[l], peer_v[l]).reshape(B, C + S, D)
            x_ctx = x_ctx + mod_c[5] * f[:, :C]
            x_lat = x_lat + mod[:, 5] * f[:, C:]
        else:
            f = peer(h2_l.reshape(B * S, D), peer_q[l], peer_keys[l], peer_u[l], peer_v[l]).reshape(B, S, D)
            x_lat = x_lat + mod[:, 5] * f
    return x_lat
```

```python
import functools
import math

import jax
import jax.numpy as jnp
import numpy as np
from jax import lax
from jax.experimental import pallas as pl
from jax.experimental.pallas import tpu as pltpu

F32 = jnp.float32
BF16 = jnp.bfloat16

GRID_W = 64
EPS = 1e-6
ROPE_BASE = 10000.0
NEG_INF = -1e30
DA_HEADS = 8
DA_DH = 64
NA_HEADS = 8
NA_DH = 128
NA_ROWS_MAX = 8
NA_COLS = 16
RK_HEADS = 16
RK_DH = 64
RK_CONV = 3
RK_LN_EPS = 64e-5
PEER_HEADS = 8
PEER_NKEYS = 128
PEER_TOPK = 16
N_MOD = 6

LANES = 128
ROW_BLOCK = 256
RK_CHUNK = 64
NA_RB = 4
VMEM_LIMIT = 56 * 1024 * 1024


def _cp(*sem):
    return pltpu.CompilerParams(dimension_semantics=sem, vmem_limit_bytes=VMEM_LIMIT)


def _split(x):
    hi = x.astype(BF16)
    lo = (x - hi.astype(F32)).astype(BF16)
    return hi, lo


def _dg(a, b, ca, cb):
    return lax.dot_general(a, b, (((ca,), (cb,)), ((), ())), preferred_element_type=F32)


def _dot1(a, b, ca=1, cb=0):
    return _dg(a.astype(BF16), b.astype(BF16), ca, cb)


def _mm_kernel(a_ref, b_ref, o_ref):
    o_ref[...] = _dot1(a_ref[...], b_ref[...]).astype(o_ref.dtype)


def matmul(a, b, out_dtype=F32, tm=1024, tn=512):
    M, K = a.shape
    N = b.shape[1]
    tm = min(tm, M)
    tn = min(tn, N)
    assert M % tm == 0 and N % tn == 0, (M, N, tm, tn)
    return pl.pallas_call(
        _mm_kernel,
        grid=(M // tm, N // tn),
        in_specs=[pl.BlockSpec((tm, K), lambda i, j: (i, 0)), pl.BlockSpec((K, tn), lambda i, j: (0, j))],
        out_specs=pl.BlockSpec((tm, tn), lambda i, j: (i, j)),
        out_shape=jax.ShapeDtypeStruct((M, N), out_dtype),
        compiler_params=_cp("parallel", "arbitrary"),
        name="matmul",
    )(a, b)


def _norm_mod_kernel(x_ref, g_ref, sh_ref, sc_ref, o_ref):
    x = x_ref[0].astype(F32)
    n = x * lax.rsqrt(jnp.mean(x * x, axis=-1, keepdims=True) + EPS) * g_ref[...]
    o_ref[0] = (n * (1.0 + sc_ref[0, 0]) + sh_ref[0, 0]).astype(o_ref.dtype)


def norm_mod(x, g, shift, scale, out_dtype):
    B, T, D = x.shape
    nb = T // ROW_BLOCK
    mspec = pl.BlockSpec((1, 1, 1, D), lambda b, i: (b, i, 0, 0))
    return pl.pallas_call(
        _norm_mod_kernel,
        grid=(B, nb),
        in_specs=[pl.BlockSpec((1, ROW_BLOCK, D), lambda b, i: (b, i, 0)),
                  pl.BlockSpec((1, D), lambda b, i: (0, 0)), mspec, mspec],
        out_specs=pl.BlockSpec((1, ROW_BLOCK, D), lambda b, i: (b, i, 0)),
        out_shape=jax.ShapeDtypeStruct((B, T, D), out_dtype),
        compiler_params=_cp("parallel", "parallel"),
        name="norm_mod",
    )(x, g.reshape(1, D), shift, scale)


def _half_rmsnorm(x, gain):
    lane = lax.broadcasted_iota(jnp.int32, x.shape, 1)
    lo = lane < DA_DH
    x2 = x * x
    s_lo = jnp.sum(jnp.where(lo, x2, 0.0), axis=-1, keepdims=True)
    s_hi = jnp.sum(jnp.where(lo, 0.0, x2), axis=-1, keepdims=True)
    inv = jnp.where(lo, lax.rsqrt(s_lo * (1.0 / DA_DH) + EPS), lax.rsqrt(s_hi * (1.0 / DA_DH) + EPS))
    return x * inv * gain


def _rope(x, cos, sin_signed):
    lane = lax.broadcasted_iota(jnp.int32, x.shape, 1)
    first = (lane % (DA_DH // 2)) < (DA_DH // 4)
    rot = jnp.where(first, pltpu.roll(x, LANES - DA_DH // 4, 1), pltpu.roll(x, DA_DH // 4, 1))
    return x * cos + rot * sin_signed


def _diff_attn_kernel(lam_ref, q_ref, k_ref, v_ref, cq_ref, sq_ref, ck_ref, sk_ref, qg_ref, kg_ref, sg_ref,
                      o_ref, kn_ref, vb_ref, *, n_lat, post_scale):
    i = pl.program_id(2)

    @pl.when(i == 0)
    def _():
        kn = _rope(_half_rmsnorm(k_ref[0].astype(F32), kg_ref[...]), ck_ref[...], sk_ref[...])
        kn_ref[...] = kn.astype(BF16)
        vb_ref[...] = v_ref[0].astype(BF16)

    q = _rope(_half_rmsnorm(q_ref[0].astype(F32), qg_ref[...]), cq_ref[...], sq_ref[...]) * (DA_DH ** -0.5)
    lane = lax.broadcasted_iota(jnp.int32, q.shape, 1)
    kn = kn_ref[...]
    s1 = _dg(jnp.where(lane < DA_DH, q, 0.0).astype(BF16), kn, 1, 1)
    s2 = _dg(jnp.where(lane < DA_DH, 0.0, q).astype(BF16), kn, 1, 1)
    kpos = lax.broadcasted_iota(jnp.int32, s1.shape, 1)
    hide = jnp.logical_and(i * ROW_BLOCK >= n_lat, kpos < n_lat)
    s1 = jnp.where(hide, NEG_INF, s1)
    s2 = jnp.where(hide, NEG_INF, s2)

    def softmax(s):
        e = jnp.exp(s - jnp.max(s, axis=-1, keepdims=True))
        return e / jnp.sum(e, axis=-1, keepdims=True)

    a = softmax(s1) - lam_ref[0, 0] * softmax(s2)
    o = _dg(a.astype(BF16), vb_ref[...], 1, 0)
    o = o * lax.rsqrt(jnp.mean(o * o, axis=-1, keepdims=True) + EPS) * sg_ref[...] * post_scale
    o_ref[0] = o.astype(o_ref.dtype)


def diff_attention(proj, lam, cos, sin, q_g, k_g, sub_g, lam_init, n_lat, n_q):
    B, T, _ = proj.shape
    H = DA_HEADS
    tile2 = lambda g: jnp.tile(g, LANES // g.shape[0]).reshape(1, LANES)
    row = lambda: pl.BlockSpec((1, LANES), lambda b, h, i: (0, 0))
    return pl.pallas_call(
        functools.partial(_diff_attn_kernel, n_lat=n_lat, post_scale=1.0 - lam_init),
        grid=(B, H, n_q // ROW_BLOCK),
        in_specs=[pl.BlockSpec(memory_space=pltpu.SMEM),
                  pl.BlockSpec((1, ROW_BLOCK, LANES), lambda b, h, i: (b, i, h)),
                  pl.BlockSpec((1, T, LANES), lambda b, h, i: (b, 0, H + h)),
                  pl.BlockSpec((1, T, LANES), lambda b, h, i: (b, 0, 2 * H + h)),
                  pl.BlockSpec((ROW_BLOCK, LANES), lambda b, h, i: (i, 0)),
                  pl.BlockSpec((ROW_BLOCK, LANES), lambda b, h, i: (i, 0)),
                  pl.BlockSpec((T, LANES), lambda b, h, i: (0, 0)),
                  pl.BlockSpec((T, LANES), lambda b, h, i: (0, 0)),
                  row(), row(), row()],
        out_specs=pl.BlockSpec((1, ROW_BLOCK, LANES), lambda b, h, i: (b, i, h)),
        out_shape=jax.ShapeDtypeStruct((B, n_q, H * LANES), F32),
        scratch_shapes=[pltpu.VMEM((T, LANES), BF16), pltpu.VMEM((T, LANES), BF16)],
        compiler_params=_cp("parallel", "parallel", "arbitrary"),
        name="diff_attention",
    )(lam.reshape(1, 1).astype(F32), proj, proj, proj, cos, sin, cos, sin, tile2(q_g), tile2(k_g),
      sub_g.reshape(1, LANES))


def rope_tables(n_lat, n_ctx):
    t = np.arange(n_lat)
    n_freq = DA_DH // 4
    inv = ROPE_BASE ** (-np.arange(n_freq, dtype=np.float64) / n_freq)
    ang_r = (t // GRID_W)[:, None] * inv
    ang_c = (t % GRID_W)[:, None] * inv
    ang = np.concatenate([ang_r, ang_r, ang_c, ang_c], axis=-1)
    sign = np.tile(np.concatenate([-np.ones(n_freq), np.ones(n_freq)]), 2)
    cos = np.concatenate([np.cos(ang), np.ones((n_ctx, DA_DH))], axis=0)
    sin = np.concatenate([np.sin(ang) * sign, np.zeros((n_ctx, DA_DH))], axis=0)
    return (jnp.asarray(np.tile(cos, (1, 2)), F32), jnp.asarray(np.tile(sin, (1, 2)), F32))


def _na_window_rows(rows):
    return min(NA_ROWS_MAX, rows) + NA_RB - 1


def _na_kernel(q_ref, k_ref, v_ref, bias_ref, qg_ref, kg_ref, o_ref, kn_ref, vb_ref, *, n_lat, rows, wr):
    i = pl.program_id(2)

    def rms(x, g):
        return x * lax.rsqrt(jnp.mean(x * x, axis=-1, keepdims=True) + EPS) * g

    @pl.when(i == 0)
    def _():
        kn_ref[...] = rms(k_ref[0].astype(F32), kg_ref[...]).astype(BF16)
        vb_ref[...] = v_ref[0].astype(BF16)

    kr = min(NA_ROWS_MAX, rows)
    ws = jnp.clip(i * NA_RB - kr // 2, 0, rows - wr)
    start = pl.multiple_of(ws * GRID_W, GRID_W)
    q = (rms(q_ref[0].astype(F32), qg_ref[...]) * (NA_DH ** -0.5)).astype(BF16)
    kw = kn_ref[pl.ds(start, wr * GRID_W), :]
    vw = vb_ref[pl.ds(start, wr * GRID_W), :]
    kc = kn_ref[n_lat:, :]
    vc = vb_ref[n_lat:, :]
    s_w = _dg(q, kw, 1, 1) + bias_ref[0, 0]
    s_c = _dg(q, kc, 1, 1)
    m = jnp.maximum(jnp.max(s_w, axis=-1, keepdims=True), jnp.max(s_c, axis=-1, keepdims=True))
    p_w = jnp.exp(s_w - m)
    p_c = jnp.exp(s_c - m)
    den = jnp.sum(p_w, axis=-1, keepdims=True) + jnp.sum(p_c, axis=-1, keepdims=True)
    o = _dg((p_w / den).astype(BF16), vw, 1, 0) + _dg((p_c / den).astype(BF16), vc, 1, 0)
    o_ref[0] = o.astype(o_ref.dtype)


def na_bias_table(rpb, n_lat, with_ctx):
    rows = n_lat // GRID_W
    kr = min(NA_ROWS_MAX, rows)
    wr = _na_window_rows(rows)
    nblk = rows // NA_RB
    qc = np.arange(GRID_W)[:, None]
    kc = np.arange(GRID_W)[None, :]
    win0 = np.clip(qc - NA_COLS // 2, 0, GRID_W - NA_COLS)
    col_ok = (kc >= win0) & (kc < win0 + NA_COLS)
    cidx = np.clip(kc - qc + NA_COLS - 1, 0, 2 * NA_COLS - 2)
    cols = jnp.where(col_ok, rpb[:, :, cidx], NEG_INF)
    blk = np.arange(nblk)[:, None, None]
    r = blk * NA_RB + np.arange(NA_RB)[None, :, None]
    ws = np.clip(blk * NA_RB - kr // 2, 0, rows - wr)
    kabs = ws + np.arange(wr)[None, None, :]
    rs = np.clip(r - kr // 2, 0, rows - kr)
    row_ok = (kabs >= rs) & (kabs < rs + kr)
    ridx = np.clip(kabs - r + NA_ROWS_MAX - 1, 0, 2 * NA_ROWS_MAX - 2)
    tab = jnp.take(cols, ridx.reshape(-1), axis=1).reshape(NA_HEADS, nblk, NA_RB, wr, GRID_W, GRID_W)
    tab = jnp.where(row_ok[None, :, :, :, None, None], tab, NEG_INF)
    tab = jnp.transpose(tab, (1, 0, 2, 4, 3, 5)).reshape(nblk, NA_HEADS, NA_RB * GRID_W, wr * GRID_W)
    if with_ctx:
        tab = jnp.concatenate([tab, jnp.full((1,) + tab.shape[1:], NEG_INF, F32)], axis=0)
    return tab.astype(F32)


def neighbourhood_attention(proj, bias, q_g, k_g, n_lat, n_q, col0):
    B, T, _ = proj.shape
    H = NA_HEADS
    rows = n_lat // GRID_W
    wr = _na_window_rows(rows)
    qb = NA_RB * GRID_W
    row = lambda: pl.BlockSpec((1, LANES), lambda b, h, i: (0, 0))
    return pl.pallas_call(
        functools.partial(_na_kernel, n_lat=n_lat, rows=rows, wr=wr),
        grid=(B, H, n_q // qb),
        in_specs=[pl.BlockSpec((1, qb, LANES), lambda b, h, i: (b, i, col0 + h)),
                  pl.BlockSpec((1, T, LANES), lambda b, h, i: (b, 0, col0 + H + h)),
                  pl.BlockSpec((1, T, LANES), lambda b, h, i: (b, 0, col0 + 2 * H + h)),
                  pl.BlockSpec((1, 1, qb, wr * GRID_W), lambda b, h, i: (i, h, 0, 0)),
                  row(), row()],
        out_specs=pl.BlockSpec((1, qb, LANES), lambda b, h, i: (b, i, h)),
        out_shape=jax.ShapeDtypeStruct((B, n_q, H * LANES), F32),
        scratch_shapes=[pltpu.VMEM((T, LANES), BF16), pltpu.VMEM((T, LANES), BF16)],
        compiler_params=_cp("parallel", "parallel", "arbitrary"),
        name="neighbourhood_attention",
    )(proj, proj, proj, bias, q_g.reshape(1, LANES), k_g.reshape(1, LANES))


def _rwkv_kernel(r_ref, v_ref, kk_ref, lw_ref, k_ref, a_ref, y_ref, st_ref, *, nh):
    L = RK_CHUNK
    N = RK_DH

    @pl.when(pl.program_id(3) == 0)
    def _():
        st_ref[...] = jnp.zeros_like(st_ref)

    fwd = pl.program_id(0) == 0
    sgn = 1 - 2 * pl.program_id(0)
    row = lax.broadcasted_iota(jnp.int32, (L, L), 0)
    col = lax.broadcasted_iota(jnp.int32, (L, L), 1)
    ahead = (row - col) * sgn
    tril = ahead >= 0
    stril = ahead > 0
    tril_b = tril.astype(BF16)
    eye = (row == col).astype(F32)
    rn = lax.broadcasted_iota(jnp.int32, (N, N), 0)
    cn = lax.broadcasted_iota(jnp.int32, (N, N), 1)
    eye_n = (rn == cn).astype(F32)

    hs = range(nh)
    sls = [slice(h * N, (h + 1) * N) for h in hs]
    each = lambda f, *cols: [f(*args) for args in zip(*cols)]
    r = [r_ref[0, :, sl] for sl in sls]
    v = [v_ref[0, :, sl] for sl in sls]
    kk = [kk_ref[0, :, sl] for sl in sls]
    lw = [lw_ref[0, 0, :, sl] for sl in sls]
    k = [k_ref[0, 0, :, sl] for sl in sls]
    a = [a_ref[0, 0, :, sl] for sl in sls]

    def cumsum(z):
        hi, lo = _split(z)
        return _dg(tril_b, hi, 1, 0) + _dg(tril_b, lo, 1, 0)

    cum = each(cumsum, lw)
    cum_l = each(lambda c: jnp.where(fwd, c[L - 1:L, :], c[0:1, :]), cum)
    ka = each(lambda x, y: x * y, kk, a)
    at = each(lambda x, c, z: -x * jnp.exp(c - z), kk, cum, lw)
    p_inv = each(lambda c: jnp.exp(-c), cum)
    bt = each(lambda x, p: x * p, ka, p_inv)
    kt = each(lambda x, p: x * p, k, p_inv)
    rt = each(lambda x, c: x * jnp.exp(c), r, cum)
    p_end = each(lambda cl, c: jnp.exp(cl - c), cum_l, cum)
    bh = each(lambda x, p: x * p, ka, p_end)
    kh = each(lambda x, p: x * p, k, p_end)
    x = each(lambda p, q: jnp.concatenate([p, q], axis=0), at, rt)
    xb = each(lambda p, q: _dot1(p, q, 1, 1), x, bt)
    xk = each(lambda p, q: _dot1(p, q, 1, 1), x, kt)
    a_ab = each(lambda z: jnp.where(stril, z[:L], 0.0), xb)
    a_ak = each(lambda z: jnp.where(stril, z[:L], 0.0), xk)
    a_rb = each(lambda z: jnp.where(tril, z[L:], 0.0), xb)
    a_rk = each(lambda z: jnp.where(tril, z[L:], 0.0), xk)
    xp = a_ab
    tm = each(lambda z: eye + z, a_ab)
    for _ in range(int(math.log2(L)) - 1):
        xp = each(lambda z: _dot1(z, z), xp)
        tm = each(lambda t, z: t + _dot1(t, z), tm, xp)
    v2 = each(_dot1, a_ak, v)
    w1 = each(_dot1, tm, at)
    w2 = each(_dot1, tm, v2)
    qp = each(lambda p, q, z: p + _dot1(q, z), rt, a_rb, w1)
    yp = each(lambda q, z, p, u: _dot1(q, z) + _dot1(p, u), a_rb, w2, a_rk, v)
    mm = each(lambda cl, p, z: eye_n * jnp.exp(cl) + _dot1(p, z, 0, 0), cum_l, bh, w1)
    nn = each(lambda p, z, q, u: _dot1(p, z, 0, 0) + _dot1(q, u, 0, 0), bh, w2, kh, v)
    st = [st_ref[h] for h in hs]
    y = each(lambda q, s_, p: _dot1(q, s_) + p, qp, st, yp)
    st_new = each(lambda m_, s_, n_: _dot1(m_, s_) + n_, mm, st, nn)
    for h in hs:
        y_ref[0, 0, :, sls[h]] = y[h]
        st_ref[h] = st_new[h]


def rwkv_scan(r, v, kk, lw, k, a, n_lat, nh=16):
    D2, B, T, C = lw.shape
    wb = nh * RK_DH
    n_all = T // RK_CHUNK
    n_l = n_lat // RK_CHUNK

    def chunk(d, c):
        return jnp.where(d == 0, (c + n_l) % n_all, n_all - 1 - c)

    shared = pl.BlockSpec((1, RK_CHUNK, wb), lambda d, b, h, c: (b, chunk(d, c), h))
    perdir = pl.BlockSpec((1, 1, RK_CHUNK, wb), lambda d, b, h, c: (d, b, chunk(d, c), h))
    return pl.pallas_call(
        functools.partial(_rwkv_kernel, nh=nh),
        grid=(D2, B, C // wb, n_all),
        in_specs=[shared, shared, shared, perdir, perdir, perdir],
        out_specs=perdir,
        out_shape=jax.ShapeDtypeStruct(lw.shape, F32),
        scratch_shapes=[pltpu.VMEM((nh, RK_DH, RK_DH), F32)],
        compiler_params=_cp("parallel", "parallel", "parallel", "arbitrary"),
        name="rwkv_scan",
    )(r, v, kk, lw, k, a)


def _first_max(vals, idx, big):
    m = jnp.max(vals, axis=0, keepdims=True)
    first = jnp.min(jnp.where(vals == m, idx, big), axis=0, keepdims=True)
    return m, idx == first


def _peer_select_kernel(q_ref, keys_ref, rank2_ref, e2_ref, cnt1_ref, e1_ref):
    K = PEER_TOPK
    nk = PEER_NKEYS
    q = q_ref[...]
    tt = q.shape[0]
    idx = lax.broadcasted_iota(jnp.int32, (nk, tt), 0)
    idk = lax.broadcasted_iota(jnp.int32, (K, tt), 0)

    def top(p):
        s = _dot1(keys_ref[0, p], q[:, p * nk:(p + 1) * nk], 1, 1)

        def body(j, c):
            vals, rank, topv = c
            m, sel = _first_max(vals, idx, nk)
            return (jnp.where(sel, -jnp.inf, vals), jnp.where(sel, j, rank), jnp.where(idk == j, m, topv))

        _, rank, topv = lax.fori_loop(0, K, body, (s, jnp.full((nk, tt), K, jnp.int32), jnp.zeros((K, tt), F32)))
        return s, rank, topv

    s1, rank1, top1 = top(0)
    s2, rank2, top2 = top(1)

    def merge(j, c):
        ptr, z = c
        b_sel = jnp.zeros((K, tt), F32)
        for t in range(K):
            b_sel = jnp.where(ptr == t, top2[t:t + 1, :], b_sel)
        cand = jnp.where(ptr < K, top1 + b_sel, -jnp.inf)
        m, sel = _first_max(cand, idk, K)
        return ptr + sel.astype(jnp.int32), z + jnp.exp(m - (top1[0:1, :] + top2[0:1, :]))

    ptr, z = lax.fori_loop(0, K, merge, (jnp.zeros((K, tt), jnp.int32), jnp.zeros((1, tt), F32)))
    cnt1 = jnp.zeros((nk, tt), jnp.int32)
    for t in range(K):
        cnt1 = jnp.where(rank1 == t, ptr[t:t + 1, :], cnt1)
    rank2_ref[...] = rank2.astype(F32).astype(BF16)
    cnt1_ref[...] = cnt1.astype(F32)
    e2_ref[...] = jnp.where(rank2 < K, jnp.exp(s2 - top2[0:1, :]), 0.0).astype(BF16)
    e1_ref[...] = jnp.where(rank1 < K, jnp.exp(s1 - top1[0:1, :]) / z, 0.0)


def peer_select(q, keys, tt=256):
    T = q.shape[0]
    H = PEER_HEADS
    nk = PEER_NKEYS
    ospec = pl.BlockSpec((nk, tt), lambda i, h: (h, i))
    oshape = lambda dt: jax.ShapeDtypeStruct((H * nk, T), dt)
    return pl.pallas_call(
        _peer_select_kernel,
        grid=(T // tt, H),
        in_specs=[pl.BlockSpec((tt, 2 * nk), lambda i, h: (i, h)),
                  pl.BlockSpec((1, 2, nk, nk), lambda i, h: (h, 0, 0, 0))],
        out_specs=[ospec] * 4,
        out_shape=[oshape(BF16), oshape(BF16), oshape(F32), oshape(F32)],
        compiler_params=_cp("parallel", "parallel"),
        name="peer_select",
    )(q, keys)


def _gelu_tanh(x):
    return 0.5 * x * (1.0 + jnp.tanh(math.sqrt(2.0 / math.pi) * (x + 0.044715 * (x * x * x))))


def _peer_dense_kernel(xt_ref, u_ref, vt_ref, rank2_ref, e2_ref, cnt1_ref, e1_ref, o_ref, w_ref, *, te):
    e = pl.program_id(1)
    nk = PEER_NKEYS

    @pl.when(e == 0)
    def _():
        o_ref[...] = jnp.zeros_like(o_ref)

    xt = xt_ref[...]
    for cc in range(te // nk):
        c = e * (te // nk) + cc
        act = _gelu_tanh(_dg(u_ref[cc * nk:(cc + 1) * nk, :], xt, 1, 0))
        g = jnp.zeros(act.shape, BF16)
        for h in range(PEER_HEADS):
            cnt = cnt1_ref[pl.ds(h * nk + c, 1), :].astype(BF16)
            e1 = e1_ref[pl.ds(h * nk + c, 1), :].astype(BF16)
            sel = rank2_ref[h * nk:(h + 1) * nk, :] < cnt
            g = g + jnp.where(sel, e2_ref[h * nk:(h + 1) * nk, :], jnp.zeros((), BF16)) * e1
        w_ref[cc * nk:(cc + 1) * nk, :] = g * act.astype(BF16)
    o_ref[...] += _dg(vt_ref[...], w_ref[...], 1, 0)


def peer_dense(xt, u, vt, rank2, e2, cnt1, e1, tm=512, te=1024):
    D, T = xt.shape
    E = u.shape[0]
    tm = min(tm, T)
    sel = pl.BlockSpec((rank2.shape[0], tm), lambda i, e: (0, i))
    return pl.pallas_call(
        functools.partial(_peer_dense_kernel, te=te),
        grid=(T // tm, E // te),
        in_specs=[pl.BlockSpec((D, tm), lambda i, e: (0, i)),
                  pl.BlockSpec((te, D), lambda i, e: (e, 0)),
                  pl.BlockSpec((D, te), lambda i, e: (0, e)),
                  sel, sel, sel, sel],
        out_specs=pl.BlockSpec((D, tm), lambda i, e: (0, i)),
        out_shape=jax.ShapeDtypeStruct((D, T), F32),
        scratch_shapes=[pltpu.VMEM((te, tm), BF16)],
        compiler_params=_cp("parallel", "arbitrary"),
        name="peer_dense",
    )(xt, u, vt, rank2, e2, cnt1, e1)


def peer(h2, q_w, keys, u_bf, vt_bf):
    q = matmul(h2, q_w)
    rank2, e2, cnt1, e1 = peer_select(q, keys)
    out_t = peer_dense(h2.T.astype(BF16), u_bf, vt_bf, rank2, e2, cnt1, e1)
    return out_t.T


def _seq_conv(x, w, n_lat):
    def one(z):
        zp = jnp.pad(z, ((0, 0), (1, 1), (0, 0)))
        t = z.shape[1]
        return zp[:, 0:t] * w[0] + zp[:, 1:t + 1] * w[1] + zp[:, 2:t + 2] * w[2]
    return jnp.concatenate([one(x[:, :n_lat]), one(x[:, n_lat:])], axis=1)


def rwkv_mixer(h, rkv, n_lat, conv_w, w0, w1, w2, a0, a1, a2, g1, g2, k_k, k_a, r_k, ln_w, ln_b):
    B, T, D = h.shape
    H, N = RK_HEADS, RK_DH
    wc = H * N
    dr = w1.shape[-1]
    ar = a1.shape[-1]
    h2d = h.reshape(B * T, D)
    lowrank = jnp.concatenate([w1[0], w1[1], a1[0], a1[1], g1], axis=1).astype(BF16)
    lr = matmul(h2d, lowrank, tn=lowrank.shape[1])
    r, k, v = jnp.split(_seq_conv(rkv, conv_w, n_lat), 3, axis=-1)
    lws, a_s = [], []
    for d in range(2):
        lw_in = matmul(jnp.tanh(lr[:, d * dr:(d + 1) * dr]), w2[d]).reshape(B, T, wc) + w0[d]
        lws.append(-jnp.exp(-jax.nn.softplus(-lw_in) - 0.5))
        a_in = matmul(lr[:, 2 * dr + d * ar:2 * dr + (d + 1) * ar], a2[d]).reshape(B, T, wc) + a0[d]
        a_s.append(jax.nn.sigmoid(a_in))
    g = matmul(jax.nn.sigmoid(lr[:, 2 * dr + 2 * ar:]), g2).reshape(B, T, wc)
    kk = (k * k_k).reshape(B, T, H, N)
    kk = (kk * lax.rsqrt(jnp.sum(kk * kk, axis=-1, keepdims=True) + 1e-12)).reshape(B, T, wc)
    kd = [k * (1.0 + (a_s[d] - 1.0) * k_a) for d in range(2)]
    ys = rwkv_scan(r, v, kk, jnp.stack(lws), jnp.stack(kd), jnp.stack(a_s), n_lat)
    y = (ys[0] + ys[1]).reshape(B, T, H, N)
    mu = jnp.mean(y, axis=-1, keepdims=True)
    var = jnp.mean(jnp.square(y - mu), axis=-1, keepdims=True)
    y = (y - mu) * lax.rsqrt(var + RK_LN_EPS) * ln_w.reshape(H, N) + ln_b.reshape(H, N)
    rh = r.reshape(B, T, H, N)
    bonus = jnp.sum(rh * (kd[0] + kd[1]).reshape(B, T, H, N) * r_k, axis=-1, keepdims=True) * v.reshape(B, T, H, N)
    return (y + bonus).reshape(B, T, wc) * g


def kernel(x, c, ctx, c_ctx, ada_w, ada_b, norm1_g, norm2_g, w_in, da_qn, da_kn, da_lq1, da_lk1, da_lq2, da_lk2, da_subln, na_qn, na_kn, na_rpb, rk_conv, rk_w0, rk_w1, rk_w2, rk_a0, rk_a1, rk_a2, rk_g1, rk_g2, rk_kk, rk_ka, rk_rk, rk_lnw, rk_lnb, w_gate, b_gate, w_br, w_o, peer_q, peer_keys, peer_u, peer_v):
    B, S, D = x.shape
    C = ctx.shape[1]
    T = S + C
    depth = ada_w.shape[0]
    nblk = T // ROW_BLOCK
    n_lat_blk = S // ROW_BLOCK
    assert C == NA_RB * GRID_W and C % ROW_BLOCK == 0 and S % ROW_BLOCK == 0
    cos, sin = rope_tables(S, C)
    xs = jnp.concatenate([x, ctx], axis=1)
    cond = jnp.concatenate([jax.nn.silu(c), jax.nn.silu(c_ctx)[None], jnp.zeros((8 - B - 1, D), F32)], axis=0)
    wa = 3 * DA_HEADS
    for l in range(depth):
        need_ctx = l < depth - 1
        lam_init = 0.8 - 0.6 * math.exp(-0.3 * l)
        lam = jnp.exp(jnp.sum(da_lq1[l] * da_lk1[l])) - jnp.exp(jnp.sum(da_lq2[l] * da_lk2[l])) + lam_init
        mod = (matmul(cond, ada_w[l], tn=1024) + ada_b[l]).reshape(8, N_MOD, D)
        mods = jnp.concatenate([jnp.broadcast_to(mod[:B, None], (B, n_lat_blk, N_MOD, D)),
                                jnp.broadcast_to(mod[B][None, None], (B, nblk - n_lat_blk, N_MOD, D))], axis=1)
        m = [mods[:, :, j][:, :, None, :] for j in range(N_MOD)]

        def gated_add(xs, gate, f):
            return (xs.reshape(B, nblk, ROW_BLOCK, D) + gate * f.reshape(B, nblk, ROW_BLOCK, D)).reshape(B, T, D)

        h = norm_mod(xs, norm1_g[l], m[0], m[1], BF16)
        h2d = h.reshape(B * T, D)
        proj = matmul(h2d, w_in[l].astype(BF16)).reshape(B, T, -1)
        n_q = T if need_ctx else S
        ya = diff_attention(proj, lam, cos, sin, da_qn[l], da_kn[l], da_subln[l], lam_init, S, n_q)
        yb = neighbourhood_attention(proj, na_bias_table(na_rpb[l], S, need_ctx), na_qn[l], na_kn[l], S, n_q, wa)
        yc = rwkv_mixer(h, proj[:, :, 2 * wa * LANES:], S, rk_conv[l], rk_w0[l], rk_w1[l], rk_w2[l], rk_a0[l],
                        rk_a1[l], rk_a2[l], rk_g1[l], rk_g2[l], rk_kk[l], rk_ka[l], rk_rk[l], rk_lnw[l], rk_lnb[l])
        if not need_ctx:
            pad = lambda z: jnp.concatenate([z, jnp.zeros((B, C, z.shape[-1]), z.dtype)], axis=1)
            ya, yb = pad(ya), pad(yb)
        gates = jax.nn.sigmoid(matmul(h2d, w_gate[l].astype(BF16)) + b_gate[l]).reshape(B * T, 3, D)
        mix = sum(gates[:, n] * matmul(y.reshape(B * T, -1), w_br[l, n].astype(BF16))
                  for n, y in enumerate((ya, yb, yc)))
        xs = gated_add(xs, m[2], matmul(mix, w_o[l].astype(BF16)))
        h2 = norm_mod(xs, norm2_g[l], m[3], m[4], F32)
        f = peer(h2.reshape(B * T, D), peer_q[l], peer_keys[l], peer_u[l].astype(BF16), peer_v[l].T.astype(BF16))
        xs = gated_add(xs, m[5], f)
    return xs[:, :S]
```
